```python
import math
import jax, jax.numpy as jnp
from jax import lax
import numpy as np

D_MODEL = 1024
BATCH = 16
SEQ = 2048
DEPTH = 1

CHUNK = 64
Q_BLOCK = 128
D_MIX = D_MODEL
N_DIFF_HEADS = 4
DIFF_HEAD_DIM = 64
DIFF_V_DIM = 2 * DIFF_HEAD_DIM
D_ATTN = N_DIFF_HEADS * DIFF_V_DIM
D_RNN = D_MIX - D_ATTN
N_RNN_BLOCKS = 8
RNN_BLOCK = D_RNN // N_RNN_BLOCKS
CONV_WIDTH = 4
RG_C = 8.0
ROPE_THETA = 500000.0
ROPE_DIM = DIFF_HEAD_DIM // 4
D_FF = 2816
D_IN_PROJ = 3 * D_ATTN + 2 * D_RNN
EPS = 1e-6
NEG_INF = -1e30

kernel_name = "hybrid_diffattn_rglru_macaron"


def rms_norm(x, g):
    xf = x.astype(jnp.float32)
    y = xf * lax.rsqrt(jnp.mean(xf * xf, axis=-1, keepdims=True) + EPS)
    return (y * g.astype(jnp.float32)).astype(x.dtype)


def swiglu_ffn(h, w_in, w_out):
    gu = h @ w_in
    g, u = jnp.split(gu, 2, axis=-1)
    return (jax.nn.silu(g) * u) @ w_out


def rope_tables(positions):
    inv_freq = ROPE_THETA ** (-jnp.arange(0, ROPE_DIM, 2, dtype=jnp.float32) / ROPE_DIM)
    ang = positions.astype(jnp.float32)[..., None] * inv_freq
    return jnp.cos(ang)[:, :, None, None, :], jnp.sin(ang)[:, :, None, None, :]


def apply_partial_rope(x, cos, sin):
    xf = x.astype(jnp.float32)
    half = ROPE_DIM // 2
    x1, x2, rest = xf[..., :half], xf[..., half:ROPE_DIM], xf[..., ROPE_DIM:]
    out = jnp.concatenate([x1 * cos - x2 * sin, x2 * cos + x1 * sin, rest], axis=-1)
    return out.astype(x.dtype)


def diff_attention(q, k, v, cos, sin, q_norm, k_norm, lam, subln_norm, lambda_init):
    B, S, _ = q.shape
    q = q.reshape(B, S, N_DIFF_HEADS, 2, DIFF_HEAD_DIM)
    k = k.reshape(B, S, N_DIFF_HEADS, 2, DIFF_HEAD_DIM)
    v = v.reshape(B, S, N_DIFF_HEADS, DIFF_V_DIM)
    q = apply_partial_rope(rms_norm(q, q_norm), cos, sin)
    k = apply_partial_rope(rms_norm(k, k_norm), cos, sin)
    scale = DIFF_HEAD_DIM ** -0.5
    q = q.transpose(0, 2, 3, 1, 4) * scale
    k = k.transpose(0, 2, 3, 1, 4)
    v = v.transpose(0, 2, 1, 3)
    outs = []
    for blk in range(S // Q_BLOCK):
        q0 = blk * Q_BLOCK
        q1 = q0 + Q_BLOCK
        s = jnp.einsum('bhmqd,bhmkd->bhmqk', q[:, :, :, q0:q1], k[:, :, :, :q1]).astype(jnp.float32)
        q_chunk = (q0 + jnp.arange(Q_BLOCK)) // CHUNK
        k_chunk = jnp.arange(q1) // CHUNK
        mask = k_chunk[None, :] <= q_chunk[:, None]
        s = jnp.where(mask, s, NEG_INF)
        p = jax.nn.softmax(s, axis=-1)
        a = p[:, :, 0] - lam * p[:, :, 1]
        outs.append(jnp.einsum('bhqk,bhkd->bhqd', a.astype(v.dtype), v[:, :, :q1]))
    o = jnp.concatenate(outs, axis=2)
    o = rms_norm(o, subln_norm) * (1.0 - lambda_init)
    return o.transpose(0, 2, 1, 3).reshape(B, S, D_ATTN)


def _linear_combine(e1, e2):
    a1, b1 = e1
    a2, b2 = e2
    return a1 * a2, a2 * b1 + b2


def rg_lru_branch(xr, gate, conv_w, conv_b, w_a, b_a, w_x, b_x, a_param):
    B, S, _ = xr.shape
    xc = lax.conv_general_dilated(
        xr, conv_w[:, None, :], window_strides=(1,), padding=[(CONV_WIDTH - 1, 0)],
        dimension_numbers=('NWC', 'WIO', 'NWC'), feature_group_count=D_RNN) + conv_b
    xb = xc.reshape(B, S, N_RNN_BLOCKS, RNN_BLOCK)
    r = jax.nn.sigmoid((jnp.einsum('bsnc,ncd->bsnd', xb, w_a).reshape(B, S, D_RNN) + b_a).astype(jnp.float32))
    i = jax.nn.sigmoid((jnp.einsum('bsnc,ncd->bsnd', xb, w_x).reshape(B, S, D_RNN) + b_x).astype(jnp.float32))
    log_a = -RG_C * r * jax.nn.softplus(-a_param.astype(jnp.float32))
    a = jnp.exp(log_a)
    mult = jnp.sqrt(-jnp.expm1(2.0 * log_a))
    b = mult * i * xc.astype(jnp.float32)
    _, h = lax.associative_scan(_linear_combine, (a, b), axis=1)
    y = h * jax.nn.gelu(gate.astype(jnp.float32), approximate=True)
    return y.astype(xr.dtype)


def setup_inputs(seed: int = 0) -> dict:
    key = jax.random.key(seed)
    ks = jax.random.split(key, 26)
    f32 = jnp.float32

    def nrm(k, shape, scale):
        return jax.random.normal(k, shape, f32) * scale

    def gain(k, shape):
        return 1.0 + 0.05 * jax.random.normal(k, shape, f32)

    x = jax.random.normal(ks[0], (BATCH, SEQ, D_MODEL), f32)
    start = jax.random.randint(ks[1], (BATCH, 1), 0, 8 * SEQ, dtype=jnp.int32)
    positions = start + jnp.arange(SEQ, dtype=jnp.int32)[None, :]
    a_init = jax.random.uniform(ks[21], (DEPTH, D_RNN), f32, 0.9, 0.999)
    return {
        "x": x,
        "positions": positions,
        "ffn1_norm": gain(ks[2], (DEPTH, D_MODEL)),
        "ffn1_w_in": nrm(ks[3], (DEPTH, D_MODEL, 2 * D_FF), D_MODEL ** -0.5),
        "ffn1_w_out": nrm(ks[4], (DEPTH, D_FF, D_MODEL), D_FF ** -0.5),
        "mix_norm": gain(ks[5], (DEPTH, D_MODEL)),
        "w_in_proj": nrm(ks[6], (DEPTH, D_MODEL, D_IN_PROJ), D_MODEL ** -0.5),
        "q_norm": gain(ks[7], (DEPTH, DIFF_HEAD_DIM)),
        "k_norm": gain(ks[8], (DEPTH, DIFF_HEAD_DIM)),
        "lambda_q1": nrm(ks[9], (DEPTH, DIFF_HEAD_DIM), 0.1),
        "lambda_k1": nrm(ks[10], (DEPTH, DIFF_HEAD_DIM), 0.1),
        "lambda_q2": nrm(ks[11], (DEPTH, DIFF_HEAD_DIM), 0.1),
        "lambda_k2": nrm(ks[12], (DEPTH, DIFF_HEAD_DIM), 0.1),
        "subln_norm": gain(ks[13], (DEPTH, DIFF_V_DIM)),
        "conv_w": nrm(ks[14], (DEPTH, CONV_WIDTH, D_RNN), CONV_WIDTH ** -0.5),
        "conv_b": nrm(ks[15], (DEPTH, D_RNN), 0.01),
        "rg_w_a": nrm(ks[16], (DEPTH, N_RNN_BLOCKS, RNN_BLOCK, RNN_BLOCK), RNN_BLOCK ** -0.5),
        "rg_b_a": nrm(ks[17], (DEPTH, D_RNN), 0.01),
        "rg_w_x": nrm(ks[18], (DEPTH, N_RNN_BLOCKS, RNN_BLOCK, RNN_BLOCK), RNN_BLOCK ** -0.5),
        "rg_b_x": nrm(ks[19], (DEPTH, D_RNN), 0.01),
        "rg_a_param": jnp.log(a_init) - jnp.log1p(-a_init),
        "w_out_proj": nrm(ks[20], (DEPTH, D_MIX, D_MODEL), D_MIX ** -0.5),
        "ffn2_norm": gain(ks[22], (DEPTH, D_MODEL)),
        "ffn2_w_in": nrm(ks[23], (DEPTH, D_MODEL, 2 * D_FF), D_MODEL ** -0.5),
        "ffn2_w_out": nrm(ks[24], (DEPTH, D_FF, D_MODEL), D_FF ** -0.5),
    }


def reference(x, positions, ffn1_norm, ffn1_w_in, ffn1_w_out, mix_norm, w_in_proj,
              q_norm, k_norm, lambda_q1, lambda_k1, lambda_q2, lambda_k2, subln_norm,
              conv_w, conv_b, rg_w_a, rg_b_a, rg_w_x, rg_b_x, rg_a_param, w_out_proj,
              ffn2_norm, ffn2_w_in, ffn2_w_out):
    cos, sin = rope_tables(positions)
    for l in range(DEPTH):
        lambda_init = 0.8 - 0.6 * math.exp(-0.3 * l)
        x = x + 0.5 * swiglu_ffn(rms_norm(x, ffn1_norm[l]), ffn1_w_in[l], ffn1_w_out[l])
        h = rms_norm(x, mix_norm[l])
        proj = h @ w_in_proj[l]
        q, k, v, xr, gate = jnp.split(
            proj, [D_ATTN, 2 * D_ATTN, 3 * D_ATTN, 3 * D_ATTN + D_RNN], axis=-1)
        lam = (jnp.exp(jnp.sum(lambda_q1[l].astype(jnp.float32) * lambda_k1[l].astype(jnp.float32)))
               - jnp.exp(jnp.sum(lambda_q2[l].astype(jnp.float32) * lambda_k2[l].astype(jnp.float32)))
               + lambda_init)
        y_attn = diff_attention(q, k, v, cos, sin, q_norm[l], k_norm[l], lam,
                                subln_norm[l], lambda_init)
        y_rnn = rg_lru_branch(xr, gate, conv_w[l], conv_b[l], rg_w_a[l], rg_b_a[l],
                              rg_w_x[l], rg_b_x[l], rg_a_param[l])
        x = x + jnp.concatenate([y_attn, y_rnn], axis=-1) @ w_out_proj[l]
        x = x + 0.5 * swiglu_ffn(rms_norm(x, ffn2_norm[l]), ffn2_w_in[l], ffn2_w_out[l])
    return x
```

```python
import functools
import math

import jax
import jax.numpy as jnp
from jax import lax
from jax.experimental import pallas as pl
from jax.experimental.pallas import tpu as pltpu

D_MODEL = 1024
CHUNK = 64
N_HEADS = 4
HEAD_DIM = 64
V_DIM = 2 * HEAD_DIM
D_ATTN = N_HEADS * V_DIM
D_RNN = D_MODEL - D_ATTN
N_RNN_BLOCKS = 8
CONV_WIDTH = 4
RG_C = 8.0
ROPE_THETA = 500000.0
ROPE_DIM = HEAD_DIM // 4
D_FF = 2816
EPS = 1e-6
NEG_INF = -1e30

LANES = 128
SUBLANES = 8
VMEM_LIMIT_BYTES = 56 * 1024 * 1024

TOKEN_TILE = 512
FF_CHUNK = 256
Q_TILE = 256
K_TILE = 256
TIME_TILE = 512

F32 = jnp.float32
BF16 = jnp.bfloat16


def _resident(shape):
    return pl.BlockSpec(shape, lambda *_: (0,) * len(shape), pipeline_mode=pl.Buffered(1))


def _rms(xf, gain):
    ms = jnp.mean(xf * xf, axis=-1, keepdims=True)
    return xf * lax.rsqrt(ms + EPS) * gain


def _swiglu(h, wg_ref, wu_ref, wo_ref, act_ref):
    for c in range(D_FF // FF_CHUNK):
        sl = slice(c * FF_CHUNK, (c + 1) * FF_CHUNK)
        g = jnp.dot(h, wg_ref[:, sl], preferred_element_type=F32)
        u = jnp.dot(h, wu_ref[:, sl], preferred_element_type=F32)
        act_ref[:, sl] = (g * jax.nn.sigmoid(g) * u).astype(BF16)
    return jnp.dot(act_ref[...], wo_ref[...], preferred_element_type=F32)


def _group_mean_sq(t, e_ref):
    sq = t * t
    hi = sq.astype(BF16)
    lo = (sq - hi.astype(F32)).astype(BF16)
    e = e_ref[...]
    return jnp.dot(hi, e, preferred_element_type=F32) + jnp.dot(lo, e, preferred_element_type=F32)


def _rope(t, cos_f, sin_f, first_half):
    half = ROPE_DIM // 2
    outs = []
    for j in range(t.shape[1] // LANES):
        blk = t[:, j * LANES:(j + 1) * LANES]
        nxt = pltpu.roll(blk, LANES - half, axis=1)
        prv = pltpu.roll(blk, half, axis=1)
        outs.append(blk * cos_f + jnp.where(first_half, nxt, prv) * sin_f)
    return jnp.concatenate(outs, axis=1)


def _ffn1_inproj_kernel(x_ref, pos_ref, n1_ref, wg_ref, wu_ref, wo_ref, n2_ref, wp_ref,
                        qg_ref, kg_ref, freq_ref, sgn_ref, e_ref,
                        x1_ref, q_ref, k_ref, v_ref, xr_ref, gate_ref, act_ref):
    x = x_ref[...]
    h = _rms(x, n1_ref[...]).astype(BF16)
    x1 = x + 0.5 * _swiglu(h, wg_ref, wu_ref, wo_ref, act_ref)
    x1_ref[...] = x1

    h2 = _rms(x1, n2_ref[...]).astype(BF16)

    def proj(lo, width):
        return jnp.dot(h2, wp_ref[:, lo:lo + width], preferred_element_type=F32)

    v_ref[...] = proj(2 * D_ATTN, D_ATTN).astype(BF16)
    xr_ref[...] = proj(3 * D_ATTN, D_RNN)
    gate_ref[...] = proj(3 * D_ATTN + D_RNN, D_RNN)

    ang = pos_ref[...].astype(F32) * freq_ref[...]
    cos_f = jnp.cos(ang)
    sin_f = jnp.sin(ang) * sgn_ref[...]
    lane = lax.broadcasted_iota(jnp.int32, (1, LANES), 1)
    first_half = (lane & (HEAD_DIM - 1)) < (ROPE_DIM // 2)

    q = proj(0, D_ATTN)
    q = q * lax.rsqrt(_group_mean_sq(q, e_ref) + EPS) * qg_ref[...]
    q_ref[...] = (_rope(q, cos_f, sin_f, first_half) * (HEAD_DIM ** -0.5)).astype(BF16)
    k = proj(D_ATTN, D_ATTN)
    k = k * lax.rsqrt(_group_mean_sq(k, e_ref) + EPS) * kg_ref[...]
    k_ref[...] = _rope(k, cos_f, sin_f, first_half).astype(BF16)


def _ffn1_inproj(x2d, pos2d, n1, wg, wu, wo, n2, wp, qg, kg, freq, sgn, e):
    n = x2d.shape[0]
    tm = TOKEN_TILE
    row = lambda w: pl.BlockSpec((tm, w), lambda i: (i, 0))
    return pl.pallas_call(
        _ffn1_inproj_kernel,
        grid=(n // tm,),
        in_specs=[row(D_MODEL), row(1), _resident(n1.shape), _resident(wg.shape), _resident(wu.shape),
                  _resident(wo.shape), _resident(n2.shape), _resident(wp.shape), _resident(qg.shape),
                  _resident(kg.shape), _resident(freq.shape), _resident(sgn.shape), _resident(e.shape)],
        out_specs=[row(D_MODEL), row(D_ATTN), row(D_ATTN), row(D_ATTN), row(D_RNN), row(D_RNN)],
        out_shape=[jax.ShapeDtypeStruct((n, D_MODEL), F32),
                   jax.ShapeDtypeStruct((n, D_ATTN), BF16),
                   jax.ShapeDtypeStruct((n, D_ATTN), BF16),
                   jax.ShapeDtypeStruct((n, D_ATTN), BF16),
                   jax.ShapeDtypeStruct((n, D_RNN), F32),
                   jax.ShapeDtypeStruct((n, D_RNN), F32)],
        scratch_shapes=[pltpu.VMEM((tm, D_FF), BF16)],
        compiler_params=pltpu.CompilerParams(dimension_semantics=("arbitrary",),
                                             vmem_limit_bytes=VMEM_LIMIT_BYTES),
        name="ffn1_inproj",
    )(x2d, pos2d, n1, wg, wu, wo, n2, wp, qg, kg, freq, sgn, e)


def _nt_dot(a, b):
    return lax.dot_general(a, b, (((1,), (1,)), ((), ())), preferred_element_type=F32)


def _diff_attn_kernel(lq1_ref, lk1_ref, lq2_ref, lk2_ref, sub_ref, q_ref, k_ref, v_ref, o_ref,
                      m_ref, l_ref, acc_ref, *, lambda_init, seq):
    lam = (jnp.exp(jnp.sum(lq1_ref[...] * lk1_ref[...], axis=-1, keepdims=True))
           - jnp.exp(jnp.sum(lq2_ref[...] * lk2_ref[...], axis=-1, keepdims=True))
           + lambda_init)
    lane = lax.broadcasted_iota(jnp.int32, (Q_TILE, V_DIM), 1)
    map_lanes = (lane < HEAD_DIM, lane >= HEAD_DIM)
    q_chunk = lax.broadcasted_iota(jnp.int32, (Q_TILE, K_TILE), 0) // CHUNK
    k_chunk = lax.broadcasted_iota(jnp.int32, (Q_TILE, K_TILE), 1) // CHUNK
    diag_mask = k_chunk <= q_chunk

    for qb in range(seq // Q_TILE):
        rows = slice(qb * Q_TILE, (qb + 1) * Q_TILE)
        q = q_ref[rows, :]
        q_maps = [jnp.where(sel, q, jnp.zeros_like(q)) for sel in map_lanes]

        kd = k_ref[rows, :]
        vd = v_ref[rows, :]
        for mi in range(2):
            s = jnp.where(diag_mask, _nt_dot(q_maps[mi], kd), NEG_INF)
            m = jnp.max(s, axis=-1, keepdims=True)
            p = jnp.exp(s - m)
            m_ref[mi] = m
            l_ref[mi] = jnp.sum(p, axis=-1, keepdims=True)
            acc_ref[mi] = jnp.dot(p.astype(BF16), vd, preferred_element_type=F32)

        def body(j, carry):
            start = pl.multiple_of(j * K_TILE, K_TILE)
            ks = k_ref[pl.ds(start, K_TILE), :]
            vs = v_ref[pl.ds(start, K_TILE), :]
            for mi in range(2):
                s = _nt_dot(q_maps[mi], ks)
                m_prev = m_ref[mi]
                m_new = jnp.maximum(m_prev, jnp.max(s, axis=-1, keepdims=True))
                alpha = jnp.exp(m_prev - m_new)
                p = jnp.exp(s - m_new)
                l_ref[mi] = alpha * l_ref[mi] + jnp.sum(p, axis=-1, keepdims=True)
                acc_ref[mi] = alpha * acc_ref[mi] + jnp.dot(p.astype(BF16), vs, preferred_element_type=F32)
                m_ref[mi] = m_new
            return carry

        lax.fori_loop(0, qb, body, 0)

        o = acc_ref[0] / l_ref[0] - lam * (acc_ref[1] / l_ref[1])
        o = _rms(o, sub_ref[...]) * (1.0 - lambda_init)
        o_ref[rows, :] = o.astype(BF16)


def _diff_attn(q, k, v, lq1, lk1, lq2, lk2, sub, lambda_init):
    b, s, _ = q.shape
    head = pl.BlockSpec((None, s, V_DIM), lambda bi, hi: (bi, 0, hi))
    vec = lambda a: pl.BlockSpec(a.shape, lambda bi, hi: (0, 0))
    return pl.pallas_call(
        functools.partial(_diff_attn_kernel, lambda_init=lambda_init, seq=s),
        grid=(b, N_HEADS),
        in_specs=[vec(lq1), vec(lk1), vec(lq2), vec(lk2), vec(sub), head, head, head],
        out_specs=head,
        out_shape=jax.ShapeDtypeStruct((b, s, D_ATTN), BF16),
        scratch_shapes=[pltpu.VMEM((2, Q_TILE, 1), F32), pltpu.VMEM((2, Q_TILE, 1), F32),
                        pltpu.VMEM((2, Q_TILE, V_DIM), F32)],
        compiler_params=pltpu.CompilerParams(dimension_semantics=("arbitrary", "arbitrary"),
                                             vmem_limit_bytes=VMEM_LIMIT_BYTES),
        name="diff_attn",
    )(lq1, lk1, lq2, lk2, sub, q, k, v)


def _rglru_kernel(xr_ref, gate_ref, cw_ref, cb_ref, wab_ref, ba_ref, bx_ref, ap_ref, y_ref,
                  xext_ref, a_ref, b_ref, h_ref):
    t_tile = xr_ref.shape[0]
    halo = SUBLANES

    @pl.when(pl.program_id(1) == 0)
    def _():
        xext_ref[0:halo, :] = jnp.zeros((halo, D_RNN), F32)
        h_ref[...] = jnp.zeros_like(h_ref)

    xext_ref[halo:halo + t_tile, :] = xr_ref[...]
    xc = cb_ref[...]
    for j in range(CONV_WIDTH):
        off = halo - (CONV_WIDTH - 1) + j
        xc = xc + cw_ref[j:j + 1, :] * xext_ref[off:off + t_tile, :]
    xext_ref[0:halo, :] = xext_ref[t_tile:t_tile + halo, :]

    gates = jnp.dot(xc.astype(BF16), wab_ref[...], preferred_element_type=F32)
    r = jax.nn.sigmoid(gates[:, :D_RNN] + ba_ref[...])
    i = jax.nn.sigmoid(gates[:, D_RNN:] + bx_ref[...])
    neg_ap = -ap_ref[...]
    softplus = jnp.maximum(neg_ap, 0.0) + jnp.log1p(jnp.exp(-jnp.abs(neg_ap)))
    a = jnp.exp(-RG_C * r * softplus)
    a_ref[...] = a
    b_ref[...] = jnp.sqrt(1.0 - a * a) * i * xc

    row = lax.broadcasted_iota(jnp.int32, (SUBLANES, D_RNN), 0)

    def body(g, h):
        rows = pl.ds(pl.multiple_of(g * SUBLANES, SUBLANES), SUBLANES)
        a8 = a_ref[rows, :]
        b8 = b_ref[rows, :]
        for d in (1, 2, 4):
            keep = row >= d
            b8 = b8 + a8 * jnp.where(keep, pltpu.roll(b8, d, axis=0), 0.0)
            a8 = a8 * jnp.where(keep, pltpu.roll(a8, d, axis=0), 1.0)
        h8 = b8 + a8 * h
        b_ref[rows, :] = h8
        return jnp.broadcast_to(h8[SUBLANES - 1:SUBLANES, :], (SUBLANES, D_RNN))

    h_ref[...] = lax.fori_loop(0, t_tile // SUBLANES, body, h_ref[...])
    y_ref[...] = (b_ref[...] * jax.nn.gelu(gate_ref[...], approximate=True)).astype(BF16)


def _rglru(xr, gate, cw, cb, wab, ba, bx, ap):
    b, s, _ = xr.shape
    tt = TIME_TILE
    tile = pl.BlockSpec((None, tt, D_RNN), lambda bi, ti: (bi, ti, 0))
    return pl.pallas_call(
        _rglru_kernel,
        grid=(b, s // tt),
        in_specs=[tile, tile, _resident(cw.shape), _resident(cb.shape), _resident(wab.shape),
                  _resident(ba.shape), _resident(bx.shape), _resident(ap.shape)],
        out_specs=tile,
        out_shape=jax.ShapeDtypeStruct((b, s, D_RNN), BF16),
        scratch_shapes=[pltpu.VMEM((tt + SUBLANES, D_RNN), F32), pltpu.VMEM((tt, D_RNN), F32),
                        pltpu.VMEM((tt, D_RNN), F32), pltpu.VMEM((SUBLANES, D_RNN), F32)],
        compiler_params=pltpu.CompilerParams(dimension_semantics=("arbitrary", "arbitrary"),
                                             vmem_limit_bytes=VMEM_LIMIT_BYTES),
        name="rglru",
    )(xr, gate, cw, cb, wab, ba, bx, ap)


def _outproj_ffn2_kernel(x1_ref, ya_ref, yr_ref, woa_ref, wor_ref, n_ref, wg_ref, wu_ref, wo_ref,
                         out_ref, act_ref):
    x2 = (x1_ref[...]
          + jnp.dot(ya_ref[...], woa_ref[...], preferred_element_type=F32)
          + jnp.dot(yr_ref[...], wor_ref[...], preferred_element_type=F32))
    h = _rms(x2, n_ref[...]).astype(BF16)
    out_ref[...] = x2 + 0.5 * _swiglu(h, wg_ref, wu_ref, wo_ref, act_ref)


def _outproj_ffn2(x1, ya, yr, woa, wor, nrm, wg, wu, wo):
    n = x1.shape[0]
    tm = TOKEN_TILE
    row = lambda w: pl.BlockSpec((tm, w), lambda i: (i, 0))
    return pl.pallas_call(
        _outproj_ffn2_kernel,
        grid=(n // tm,),
        in_specs=[row(D_MODEL), row(D_ATTN), row(D_RNN), _resident(woa.shape), _resident(wor.shape),
                  _resident(nrm.shape), _resident(wg.shape), _resident(wu.shape), _resident(wo.shape)],
        out_specs=row(D_MODEL),
        out_shape=jax.ShapeDtypeStruct((n, D_MODEL), F32),
        scratch_shapes=[pltpu.VMEM((tm, D_FF), BF16)],
        compiler_params=pltpu.CompilerParams(dimension_semantics=("arbitrary",),
                                             vmem_limit_bytes=VMEM_LIMIT_BYTES),
        name="outproj_ffn2",
    )(x1, ya, yr, woa, wor, nrm, wg, wu, wo)


def _block_diag(w):
    nb, c, d = w.shape
    eye = jnp.eye(nb, dtype=w.dtype)
    return (eye[:, None, :, None] * w[:, :, None, :]).reshape(nb * c, nb * d)


def kernel(x, positions, ffn1_norm, ffn1_w_in, ffn1_w_out, mix_norm, w_in_proj, q_norm, k_norm, lambda_q1, lambda_k1, lambda_q2, lambda_k2, subln_norm, conv_w, conv_b, rg_w_a, rg_b_a, rg_w_x, rg_b_x, rg_a_param, w_out_proj, ffn2_norm, ffn2_w_in, ffn2_w_out):
    b, s, d = x.shape
    depth = ffn1_norm.shape[0]
    n = b * s
    row = lambda a: a.reshape(1, -1).astype(F32)

    inv_freq = ROPE_THETA ** (-jnp.arange(0, ROPE_DIM, 2, dtype=F32) / ROPE_DIM)
    lane = jnp.arange(LANES) % HEAD_DIM
    half = ROPE_DIM // 2
    freq = jnp.where(lane < ROPE_DIM, inv_freq[lane % half], 0.0).reshape(1, LANES).astype(F32)
    sgn = jnp.where(lane < half, -1.0, 1.0).reshape(1, LANES).astype(F32)
    group = jnp.arange(D_ATTN) // HEAD_DIM
    e = jnp.where(group[:, None] == group[None, :], 1.0 / HEAD_DIM, 0.0).astype(BF16)

    pos2d = positions.reshape(n, 1)
    x2d = x.reshape(n, d)
    for l in range(depth):
        lambda_init = 0.8 - 0.6 * math.exp(-0.3 * l)
        w_in1 = ffn1_w_in[l].astype(BF16)
        w_in2 = ffn2_w_in[l].astype(BF16)
        w_op = w_out_proj[l].astype(BF16)
        wab = jnp.concatenate([_block_diag(rg_w_a[l]), _block_diag(rg_w_x[l])], axis=1).astype(BF16)

        x1, q, k, v, xr, gate = _ffn1_inproj(
            x2d, pos2d, row(ffn1_norm[l]), w_in1[:, :D_FF], w_in1[:, D_FF:], ffn1_w_out[l].astype(BF16),
            row(mix_norm[l]), w_in_proj[l].astype(BF16),
            row(jnp.tile(q_norm[l], D_ATTN // HEAD_DIM)), row(jnp.tile(k_norm[l], D_ATTN // HEAD_DIM)),
            freq, sgn, e)

        y_attn = _diff_attn(q.reshape(b, s, D_ATTN), k.reshape(b, s, D_ATTN), v.reshape(b, s, D_ATTN),
                            row(lambda_q1[l]), row(lambda_k1[l]), row(lambda_q2[l]), row(lambda_k2[l]),
                            row(subln_norm[l]), lambda_init)
        y_rnn = _rglru(xr.reshape(b, s, D_RNN), gate.reshape(b, s, D_RNN), conv_w[l].astype(F32),
                       row(conv_b[l]), wab, row(rg_b_a[l]), row(rg_b_x[l]), row(rg_a_param[l]))

        x2d = _outproj_ffn2(x1, y_attn.reshape(n, D_ATTN), y_rnn.reshape(n, D_RNN),
                            w_op[:D_ATTN], w_op[D_ATTN:], row(ffn2_norm[l]),
                            w_in2[:, :D_FF], w_in2[:, D_FF:], ffn2_w_out[l].astype(BF16))
    return x2d.reshape(b, s, d)
```

```python
import functools
import math

import jax
import jax.numpy as jnp
from jax import lax
from jax.experimental import pallas as pl
from jax.experimental.pallas import tpu as pltpu

D_MODEL = 1024
CHUNK = 64
N_HEADS = 4
HEAD_DIM = 64
V_DIM = 2 * HEAD_DIM
D_ATTN = N_HEADS * V_DIM
D_RNN = D_MODEL - D_ATTN
N_RNN_BLOCKS = 8
CONV_WIDTH = 4
RG_C = 8.0
ROPE_THETA = 500000.0
ROPE_DIM = HEAD_DIM // 4
D_FF = 2816
EPS = 1e-6
NEG_INF = -1e30
Q_SCALE = HEAD_DIM ** -0.5 * math.log2(math.e)

LANES = 128
SUBLANES = 8
VMEM_LIMIT_BYTES = 56 * 1024 * 1024

TOKEN_TILE = 512
FF_CHUNK = 256
Q_TILE = 128
K_TILE = 256
TIME_TILE = 512

F32 = jnp.float32
BF16 = jnp.bfloat16


def _resident(shape):
    return pl.BlockSpec(shape, lambda *_: (0,) * len(shape), pipeline_mode=pl.Buffered(1))


def _rms(xf, gain):
    ms = jnp.mean(xf * xf, axis=-1, keepdims=True)
    return xf * lax.rsqrt(ms + EPS) * gain


def _swiglu(h, wg_ref, wu_ref, wo_ref, act_ref):
    for c in range(D_FF // FF_CHUNK):
        sl = slice(c * FF_CHUNK, (c + 1) * FF_CHUNK)
        g = jnp.dot(h, wg_ref[:, sl], preferred_element_type=F32)
        u = jnp.dot(h, wu_ref[:, sl], preferred_element_type=F32)
        act_ref[:, sl] = (g * jax.nn.sigmoid(g) * u).astype(BF16)
    return jnp.dot(act_ref[...], wo_ref[...], preferred_element_type=F32)


def _nt_dot(a, b):
    return lax.dot_general(a, b, (((1,), (1,)), ((), ())), preferred_element_type=F32)


def _headnorm_rope(tt, gain, cos, sin):
    half = ROPE_DIM // 2
    outs = []
    for g in range(tt.shape[0] // HEAD_DIM):
        blk = tt[g * HEAD_DIM:(g + 1) * HEAD_DIM, :]
        ms = jnp.mean(blk * blk, axis=0, keepdims=True)
        blk = blk * lax.rsqrt(ms + EPS) * gain
        x1, x2 = blk[:half], blk[half:ROPE_DIM]
        outs += [x1 * cos - x2 * sin, x2 * cos + x1 * sin, blk[ROPE_DIM:]]
    return jnp.concatenate(outs, axis=0)


def _ffn1_inproj_kernel(x_ref, pos_ref, n1_ref, wg_ref, wu_ref, wo_ref, n2_ref,
                        wqt_ref, wkt_ref, wvt_ref, wrg_ref, qg_ref, kg_ref, freq_ref,
                        x1_ref, qt_ref, k_ref, vt_ref, xr_ref, gate_ref, act_ref):
    x = x_ref[...]
    h = _rms(x, n1_ref[...]).astype(BF16)
    x1 = x + 0.5 * _swiglu(h, wg_ref, wu_ref, wo_ref, act_ref)
    x1_ref[...] = x1

    h2 = _rms(x1, n2_ref[...]).astype(BF16)
    rg = jnp.dot(h2, wrg_ref[...], preferred_element_type=F32)
    xr_ref[...] = rg[:, :D_RNN]
    gate_ref[...] = rg[:, D_RNN:]
    vt_ref[...] = _nt_dot(wvt_ref[...], h2).astype(BF16)

    ang = freq_ref[...] * pos_ref[...].astype(F32)
    cos, sin = jnp.cos(ang), jnp.sin(ang)
    qt = _headnorm_rope(_nt_dot(wqt_ref[...], h2), qg_ref[...], cos, sin)
    qt_ref[...] = (qt * Q_SCALE).astype(BF16)
    kt = _headnorm_rope(_nt_dot(wkt_ref[...], h2), kg_ref[...], cos, sin)
    k_ref[...] = kt.T.astype(BF16)


def _ffn1_inproj(x, pos, n1, wg, wu, wo, n2, wqt, wkt, wvt, wrg, qg, kg, freq):
    b, s, _ = x.shape
    tm = TOKEN_TILE
    tok = lambda w: pl.BlockSpec((None, tm, w), lambda bi, ti: (bi, ti, 0))
    feat = lambda w: pl.BlockSpec((None, w, tm), lambda bi, ti: (bi, 0, ti))
    consts = (n1, wg, wu, wo, n2, wqt, wkt, wvt, wrg, qg, kg, freq)
    return pl.pallas_call(
        _ffn1_inproj_kernel,
        grid=(b, s // tm),
        in_specs=[tok(D_MODEL), feat(1)] + [_resident(a.shape) for a in consts],
        out_specs=[tok(D_MODEL), feat(D_ATTN), tok(D_ATTN), feat(D_ATTN), tok(D_RNN), tok(D_RNN)],
        out_shape=[jax.ShapeDtypeStruct((b, s, D_MODEL), F32),
                   jax.ShapeDtypeStruct((b, D_ATTN, s), BF16),
                   jax.ShapeDtypeStruct((b, s, D_ATTN), BF16),
                   jax.ShapeDtypeStruct((b, D_ATTN, s), BF16),
                   jax.ShapeDtypeStruct((b, s, D_RNN), F32),
                   jax.ShapeDtypeStruct((b, s, D_RNN), F32)],
        scratch_shapes=[pltpu.VMEM((tm, D_FF), BF16)],
        compiler_params=pltpu.CompilerParams(dimension_semantics=("arbitrary", "arbitrary"),
                                             vmem_limit_bytes=VMEM_LIMIT_BYTES),
        name="ffn1_inproj",
    )(x, pos, *consts)


def _diff_attn_kernel(lq1_ref, lk1_ref, lq2_ref, lk2_ref, sub_ref, qt_ref, k_ref, vt_ref, o_ref, s_ref,
                      *, lambda_init, seq):
    lam = (jnp.exp(jnp.sum(lq1_ref[...] * lk1_ref[...], axis=-1, keepdims=True))
           - jnp.exp(jnp.sum(lq2_ref[...] * lk2_ref[...], axis=-1, keepdims=True))
           + lambda_init)
    zeros = jnp.zeros((HEAD_DIM, Q_TILE), BF16)
    lane = lax.broadcasted_iota(jnp.int32, (CHUNK, 2 * Q_TILE), 1)
    second_chunk = (lane & CHUNK) != 0

    for i in range(seq // Q_TILE):
        cols = slice(i * Q_TILE, (i + 1) * Q_TILE)
        qt = qt_ref[:, cols]
        q_rhs = jnp.concatenate([jnp.concatenate([qt[:HEAD_DIM], zeros], axis=0),
                                 jnp.concatenate([zeros, qt[HEAD_DIM:]], axis=0)], axis=1)
        visible = i * Q_TILE
        tiles = [(j * K_TILE, K_TILE) for j in range(visible // K_TILE)]
        last = (visible // K_TILE) * K_TILE
        tiles.append((last, visible + Q_TILE - last))

        m = None
        for t, (start, size) in enumerate(tiles):
            st = jnp.dot(k_ref[start:start + size, :], q_rhs, preferred_element_type=F32)
            if t == len(tiles) - 1:
                st = jnp.concatenate(
                    [st[:size - CHUNK], jnp.where(second_chunk, st[size - CHUNK:], NEG_INF)], axis=0)
            s_ref[t, 0:size, :] = st
            tile_max = jnp.max(st, axis=0, keepdims=True)
            m = tile_max if m is None else jnp.maximum(m, tile_max)

        l = jnp.zeros((1, 2 * Q_TILE), F32)
        acc = jnp.zeros((V_DIM, 2 * Q_TILE), F32)
        for t, (start, size) in enumerate(tiles):
            p = jnp.exp2(s_ref[t, 0:size, :] - m)
            l = l + jnp.sum(p, axis=0, keepdims=True)
            acc = acc + jnp.dot(vt_ref[:, start:start + size], p.astype(BF16), preferred_element_type=F32)

        inv = 1.0 / l
        o = acc[:, :Q_TILE] * inv[:, :Q_TILE] - lam * (acc[:, Q_TILE:] * inv[:, Q_TILE:])
        ms = jnp.mean(o * o, axis=0, keepdims=True)
        o = o * lax.rsqrt(ms + EPS) * sub_ref[...] * (1.0 - lambda_init)
        o_ref[cols, :] = o.T.astype(BF16)


def _diff_attn(qt, k, vt, lq1, lk1, lq2, lk2, sub, lambda_init):
    b, s, _ = k.shape
    tok = pl.BlockSpec((None, s, V_DIM), lambda bi, hi: (bi, 0, hi))
    feat = pl.BlockSpec((None, V_DIM, s), lambda bi, hi: (bi, hi, 0))
    vec = lambda a: pl.BlockSpec(a.shape, lambda bi, hi: (0, 0))
    return pl.pallas_call(
        functools.partial(_diff_attn_kernel, lambda_init=lambda_init, seq=s),
        grid=(b, N_HEADS),
        in_specs=[vec(lq1), vec(lk1), vec(lq2), vec(lk2), vec(sub), feat, tok, feat],
        out_specs=tok,
        out_shape=jax.ShapeDtypeStruct((b, s, D_ATTN), BF16),
        scratch_shapes=[pltpu.VMEM((s // K_TILE, K_TILE, 2 * Q_TILE), F32)],
        compiler_params=pltpu.CompilerParams(dimension_semantics=("arbitrary", "arbitrary"),
                                             vmem_limit_bytes=VMEM_LIMIT_BYTES),
        name="diff_attn",
    )(lq1, lk1, lq2, lk2, sub, qt, k, vt)


def _rglru_kernel(xr_ref, gate_ref, cw_ref, cb_ref, wab_ref, ba_ref, bx_ref, ap_ref, y_ref,
                  xext_ref, a_ref, b_ref, h_ref):
    t_tile = xr_ref.shape[0]
    halo = SUBLANES

    @pl.when(pl.program_id(1) == 0)
    def _():
        xext_ref[0:halo, :] = jnp.zeros((halo, D_RNN), F32)
        h_ref[...] = jnp.zeros_like(h_ref)

    xext_ref[halo:halo + t_tile, :] = xr_ref[...]
    xc = cb_ref[...]
    for j in range(CONV_WIDTH):
        off = halo - (CONV_WIDTH - 1) + j
        xc = xc + cw_ref[j:j + 1, :] * xext_ref[off:off + t_tile, :]
    xext_ref[0:halo, :] = xext_ref[t_tile:t_tile + halo, :]

    gates = jnp.dot(xc.astype(BF16), wab_ref[...], preferred_element_type=F32)
    r = jax.nn.sigmoid(gates[:, :D_RNN] + ba_ref[...])
    i = jax.nn.sigmoid(gates[:, D_RNN:] + bx_ref[...])
    neg_ap = -ap_ref[...]
    softplus = jnp.maximum(neg_ap, 0.0) + jnp.log1p(jnp.exp(-jnp.abs(neg_ap)))
    a = jnp.exp(-RG_C * r * softplus)
    a_ref[...] = a
    b_ref[...] = jnp.sqrt(1.0 - a * a) * i * xc

    row = lax.broadcasted_iota(jnp.int32, (SUBLANES, D_RNN), 0)

    def body(g, h):
        rows = pl.ds(pl.multiple_of(g * SUBLANES, SUBLANES), SUBLANES)
        a8 = a_ref[rows, :]
        b8 = b_ref[rows, :]
        for d in (1, 2, 4):
            keep = row >= d
            b8 = b8 + a8 * jnp.where(keep, pltpu.roll(b8, d, axis=0), 0.0)
            a8 = a8 * jnp.where(keep, pltpu.roll(a8, d, axis=0), 1.0)
        h8 = b8 + a8 * h
        b_ref[rows, :] = h8
        return jnp.broadcast_to(h8[SUBLANES - 1:SUBLANES, :], (SUBLANES, D_RNN))

    h_ref[...] = lax.fori_loop(0, t_tile // SUBLANES, body, h_ref[...])
    y_ref[...] = (b_ref[...] * jax.nn.gelu(gate_ref[...], approximate=True)).astype(BF16)


def _rglru(xr, gate, cw, cb, wab, ba, bx, ap):
    b, s, _ = xr.shape
    tt = TIME_TILE
    tile = pl.BlockSpec((None, tt, D_RNN), lambda bi, ti: (bi, ti, 0))
    return pl.pallas_call(
        _rglru_kernel,
        grid=(b, s // tt),
        in_specs=[tile, tile, _resident(cw.shape), _resident(cb.shape), _resident(wab.shape),
                  _resident(ba.shape), _resident(bx.shape), _resident(ap.shape)],
        out_specs=tile,
        out_shape=jax.ShapeDtypeStruct((b, s, D_RNN), BF16),
        scratch_shapes=[pltpu.VMEM((tt + SUBLANES, D_RNN), F32), pltpu.VMEM((tt, D_RNN), F32),
                        pltpu.VMEM((tt, D_RNN), F32), pltpu.VMEM((SUBLANES, D_RNN), F32)],
        compiler_params=pltpu.CompilerParams(dimension_semantics=("arbitrary", "arbitrary"),
                                             vmem_limit_bytes=VMEM_LIMIT_BYTES),
        name="rglru",
    )(xr, gate, cw, cb, wab, ba, bx, ap)


def _outproj_ffn2_kernel(x1_ref, ya_ref, yr_ref, woa_ref, wor_ref, n_ref, wg_ref, wu_ref, wo_ref,
                         out_ref, act_ref):
    x2 = (x1_ref[...]
          + jnp.dot(ya_ref[...], woa_ref[...], preferred_element_type=F32)
          + jnp.dot(yr_ref[...], wor_ref[...], preferred_element_type=F32))
    h = _rms(x2, n_ref[...]).astype(BF16)
    out_ref[...] = x2 + 0.5 * _swiglu(h, wg_ref, wu_ref, wo_ref, act_ref)


def _outproj_ffn2(x1, ya, yr, woa, wor, nrm, wg, wu, wo):
    n = x1.shape[0]
    tm = TOKEN_TILE
    row = lambda w: pl.BlockSpec((tm, w), lambda i: (i, 0))
    return pl.pallas_call(
        _outproj_ffn2_kernel,
        grid=(n // tm,),
        in_specs=[row(D_MODEL), row(D_ATTN), row(D_RNN), _resident(woa.shape), _resident(wor.shape),
                  _resident(nrm.shape), _resident(wg.shape), _resident(wu.shape), _resident(wo.shape)],
        out_specs=row(D_MODEL),
        out_shape=jax.ShapeDtypeStruct((n, D_MODEL), F32),
        scratch_shapes=[pltpu.VMEM((tm, D_FF), BF16)],
        compiler_params=pltpu.CompilerParams(dimension_semantics=("arbitrary",),
                                             vmem_limit_bytes=VMEM_LIMIT_BYTES),
        name="outproj_ffn2",
    )(x1, ya, yr, woa, wor, nrm, wg, wu, wo)


def _block_diag(w):
    nb, c, d = w.shape
    eye = jnp.eye(nb, dtype=w.dtype)
    return (eye[:, None, :, None] * w[:, :, None, :]).reshape(nb * c, nb * d)


def kernel(x, positions, ffn1_norm, ffn1_w_in, ffn1_w_out, mix_norm, w_in_proj, q_norm, k_norm, lambda_q1, lambda_k1, lambda_q2, lambda_k2, subln_norm, conv_w, conv_b, rg_w_a, rg_b_a, rg_w_x, rg_b_x, rg_a_param, w_out_proj, ffn2_norm, ffn2_w_in, ffn2_w_out):
    b, s, d = x.shape
    depth = ffn1_norm.shape[0]
    n = b * s
    tm = TOKEN_TILE
    row = lambda a: a.reshape(1, -1).astype(F32)
    col = lambda a, width: jnp.broadcast_to(a.astype(F32)[:, None], (a.shape[0], width))

    inv_freq = ROPE_THETA ** (-jnp.arange(0, ROPE_DIM, 2, dtype=F32) / ROPE_DIM)
    freq = col(inv_freq, tm)
    pos = positions.reshape(b, 1, s)

    for l in range(depth):
        lambda_init = 0.8 - 0.6 * math.exp(-0.3 * l)
        w_in1 = ffn1_w_in[l].astype(BF16)
        w_in2 = ffn2_w_in[l].astype(BF16)
        w_ip = w_in_proj[l].astype(BF16)
        w_op = w_out_proj[l].astype(BF16)
        wab = jnp.concatenate([_block_diag(rg_w_a[l]), _block_diag(rg_w_x[l])], axis=1).astype(BF16)

        x1, qt, k, vt, xr, gate = _ffn1_inproj(
            x, pos, row(ffn1_norm[l]), w_in1[:, :D_FF], w_in1[:, D_FF:], ffn1_w_out[l].astype(BF16),
            row(mix_norm[l]), w_ip[:, :D_ATTN].T, w_ip[:, D_ATTN:2 * D_ATTN].T, w_ip[:, 2 * D_ATTN:3 * D_ATTN].T,
            w_ip[:, 3 * D_ATTN:], col(q_norm[l], tm), col(k_norm[l], tm), freq)

        y_attn = _diff_attn(qt, k, vt, row(lambda_q1[l]), row(lambda_k1[l]), row(lambda_q2[l]),
                            row(lambda_k2[l]), col(subln_norm[l], Q_TILE), lambda_init)
        y_rnn = _rglru(xr, gate, conv_w[l].astype(F32), row(conv_b[l]), wab, row(rg_b_a[l]),
                       row(rg_b_x[l]), row(rg_a_param[l]))

        x = _outproj_ffn2(x1.reshape(n, d), y_attn.reshape(n, D_ATTN), y_rnn.reshape(n, D_RNN),
                          w_op[:D_ATTN], w_op[D_ATTN:], row(ffn2_norm[l]),
                          w_in2[:, :D_FF], w_in2[:, D_FF:], ffn2_w_out[l].astype(BF16)).reshape(b, s, d)
    return x
```

```python
import functools
import math

import jax
import jax.numpy as jnp
from jax import lax
from jax.experimental import pallas as pl
from jax.experimental.pallas import tpu as pltpu

D_MODEL = 1024
CHUNK = 64
N_HEADS = 4
HEAD_DIM = 64
V_DIM = 2 * HEAD_DIM
D_ATTN = N_HEADS * V_DIM
D_RNN = D_MODEL - D_ATTN
N_RNN_BLOCKS = 8
CONV_WIDTH = 4
RG_C = 8.0
ROPE_THETA = 500000.0
ROPE_DIM = HEAD_DIM // 4
D_FF = 2816
EPS = 1e-6
NEG_INF = -1e30
Q_SCALE = HEAD_DIM ** -0.5 * math.log2(math.e)

LANES = 128
SUBLANES = 8
VMEM_LIMIT_BYTES = 56 * 1024 * 1024

TOKEN_TILE = 512
FF_CHUNK = 256
Q_TILE = 128
K_TILE = 256
ONES_ROWS = 16
SCORE_AHEAD = 2
TIME_TILE = 512

F32 = jnp.float32
BF16 = jnp.bfloat16


def _resident(shape):
    return pl.BlockSpec(shape, lambda *_: (0,) * len(shape), pipeline_mode=pl.Buffered(1))


def _rms(xf, gain):
    ms = jnp.mean(xf * xf, axis=-1, keepdims=True)
    return xf * lax.rsqrt(ms + EPS) * gain


def _swiglu(h, wg_ref, wu_ref, wo_ref, act_ref):
    for c in range(D_FF // FF_CHUNK):
        sl = slice(c * FF_CHUNK, (c + 1) * FF_CHUNK)
        g = jnp.dot(h, wg_ref[:, sl], preferred_element_type=F32)
        u = jnp.dot(h, wu_ref[:, sl], preferred_element_type=F32)
        act_ref[:, sl] = (g * jax.nn.sigmoid(g) * u).astype(BF16)
    return jnp.dot(act_ref[...], wo_ref[...], preferred_element_type=F32)


def _nt_dot(a, b):
    return lax.dot_general(a, b, (((1,), (1,)), ((), ())), preferred_element_type=F32)


def _headnorm_rope(tt, gain, cos, sin):
    half = ROPE_DIM // 2
    outs = []
    for g in range(tt.shape[0] // HEAD_DIM):
        blk = tt[g * HEAD_DIM:(g + 1) * HEAD_DIM, :]
        ms = jnp.mean(blk * blk, axis=0, keepdims=True)
        blk = blk * lax.rsqrt(ms + EPS) * gain
        x1, x2 = blk[:half], blk[half:ROPE_DIM]
        outs += [x1 * cos - x2 * sin, x2 * cos + x1 * sin, blk[ROPE_DIM:]]
    return jnp.concatenate(outs, axis=0)


def _ffn1_inproj_kernel(x_ref, pos_ref, n1_ref, wg_ref, wu_ref, wo_ref, n2_ref,
                        wqt_ref, wkt_ref, wvt_ref, wrg_ref, qg_ref, kg_ref, freq_ref,
                        x1_ref, qt_ref, k_ref, vt_ref, xr_ref, gate_ref, act_ref):
    x = x_ref[...]
    h = _rms(x, n1_ref[...]).astype(BF16)
    x1 = x + 0.5 * _swiglu(h, wg_ref, wu_ref, wo_ref, act_ref)
    x1_ref[...] = x1

    h2 = _rms(x1, n2_ref[...]).astype(BF16)
    rg = jnp.dot(h2, wrg_ref[...], preferred_element_type=F32)
    xr_ref[...] = rg[:, :D_RNN]
    gate_ref[...] = rg[:, D_RNN:]
    vt_ref[...] = _nt_dot(wvt_ref[...], h2).astype(BF16)

    ang = freq_ref[...] * pos_ref[...].astype(F32)
    cos, sin = jnp.cos(ang), jnp.sin(ang)
    qt = _headnorm_rope(_nt_dot(wqt_ref[...], h2), qg_ref[...], cos, sin)
    qt_ref[...] = (qt * Q_SCALE).astype(BF16)
    kt = _headnorm_rope(_nt_dot(wkt_ref[...], h2), kg_ref[...], cos, sin)
    k_ref[...] = kt.T.astype(BF16)


def _ffn1_inproj(x, pos, n1, wg, wu, wo, n2, wqt, wkt, wvt, wrg, qg, kg, freq):
    b, s, _ = x.shape
    tm = TOKEN_TILE
    tok = lambda w: pl.BlockSpec((None, tm, w), lambda bi, ti: (bi, ti, 0))
    feat = lambda w: pl.BlockSpec((None, w, tm), lambda bi, ti: (bi, 0, ti))
    consts = (n1, wg, wu, wo, n2, wqt, wkt, wvt, wrg, qg, kg, freq)
    return pl.pallas_call(
        _ffn1_inproj_kernel,
        grid=(b, s // tm),
        in_specs=[tok(D_MODEL), feat(1)] + [_resident(a.shape) for a in consts],
        out_specs=[tok(D_MODEL), feat(D_ATTN), tok(D_ATTN), feat(D_ATTN), tok(D_RNN), tok(D_RNN)],
        out_shape=[jax.ShapeDtypeStruct((b, s, D_MODEL), F32),
                   jax.ShapeDtypeStruct((b, D_ATTN, s), BF16),
                   jax.ShapeDtypeStruct((b, s, D_ATTN), BF16),
                   jax.ShapeDtypeStruct((b, D_ATTN, s), BF16),
                   jax.ShapeDtypeStruct((b, s, D_RNN), F32),
                   jax.ShapeDtypeStruct((b, s, D_RNN), F32)],
        scratch_shapes=[pltpu.VMEM((tm, D_FF), BF16)],
        compiler_params=pltpu.CompilerParams(dimension_semantics=("arbitrary", "arbitrary"),
                                             vmem_limit_bytes=VMEM_LIMIT_BYTES),
        name="ffn1_inproj",
    )(x, pos, *consts)


def _diff_attn_kernel(lq1_ref, lk1_ref, lq2_ref, lk2_ref, sub_ref, qt_ref, k_ref, vt_ref, o_ref, s_ref,
                      *, lambda_init, seq):
    lam = (jnp.exp(jnp.sum(lq1_ref[...] * lk1_ref[...], axis=-1, keepdims=True))
           - jnp.exp(jnp.sum(lq2_ref[...] * lk2_ref[...], axis=-1, keepdims=True))
           + lambda_init)
    zeros = jnp.zeros((HEAD_DIM, Q_TILE), BF16)
    lane = lax.broadcasted_iota(jnp.int32, (CHUNK, 2 * Q_TILE), 1)
    second_chunk = (lane & CHUNK) != 0

    def key_tiles(i):
        visible = i * Q_TILE
        tiles = [(j * K_TILE, K_TILE) for j in range(visible // K_TILE)]
        last = (visible // K_TILE) * K_TILE
        return tiles + [(last, visible + Q_TILE - last)]

    def scores(i, out):
        qt = qt_ref[:, i * Q_TILE:(i + 1) * Q_TILE]
        q_rhs = jnp.concatenate([jnp.concatenate([qt[:HEAD_DIM], zeros], axis=0),
                                 jnp.concatenate([zeros, qt[HEAD_DIM:]], axis=0)], axis=1)
        tiles = key_tiles(i)
        m = None
        for t, (start, size) in enumerate(tiles):
            st = jnp.dot(k_ref[start:start + size, :], q_rhs, preferred_element_type=F32)
            if t == len(tiles) - 1:
                st = jnp.concatenate(
                    [st[:size - CHUNK], jnp.where(second_chunk, st[size - CHUNK:], NEG_INF)], axis=0)
            s_ref[i % (SCORE_AHEAD + 1), t, 0:size, :] = st
            tile_max = jnp.max(st, axis=0, keepdims=True)
            m = tile_max if m is None else jnp.maximum(m, tile_max)
            out[0] = m
            yield

    def combine(i, m):
        acc = jnp.zeros((V_DIM + ONES_ROWS, 2 * Q_TILE), F32)
        for t, (start, size) in enumerate(key_tiles(i)):
            p = jnp.exp2(s_ref[i % (SCORE_AHEAD + 1), t, 0:size, :] - m)
            v_ones = jnp.concatenate([vt_ref[:, start:start + size], jnp.ones((ONES_ROWS, size), BF16)], axis=0)
            acc = acc + jnp.dot(v_ones, p.astype(BF16), preferred_element_type=F32)
            yield
        inv = 1.0 / acc[V_DIM:V_DIM + 1, :]
        o = (acc[:V_DIM, :Q_TILE] * inv[:, :Q_TILE]
             - lam * (acc[:V_DIM, Q_TILE:] * inv[:, Q_TILE:]))
        ms = jnp.mean(o * o, axis=0, keepdims=True)
        o = o * lax.rsqrt(ms + EPS) * sub_ref[...] * (1.0 - lambda_init)
        o_ref[i * Q_TILE:(i + 1) * Q_TILE, :] = o.T.astype(BF16)

    n_tiles = seq // Q_TILE
    maxes = {}
    for i in range(-SCORE_AHEAD, n_tiles):
        if i + SCORE_AHEAD < n_tiles:
            out = [None]
            for _ in scores(i + SCORE_AHEAD, out):
                pass
            maxes[i + SCORE_AHEAD] = out[0]
        if i >= 0:
            for _ in combine(i, maxes.pop(i)):
                pass


def _diff_attn(qt, k, vt, lq1, lk1, lq2, lk2, sub, lambda_init):
    b, s, _ = k.shape
    tok = pl.BlockSpec((None, s, V_DIM), lambda bi, hi: (bi, 0, hi))
    feat = pl.BlockSpec((None, V_DIM, s), lambda bi, hi: (bi, hi, 0))
    vec = lambda a: pl.BlockSpec(a.shape, lambda bi, hi: (0, 0))
    return pl.pallas_call(
        functools.partial(_diff_attn_kernel, lambda_init=lambda_init, seq=s),
        grid=(b, N_HEADS),
        in_specs=[vec(lq1), vec(lk1), vec(lq2), vec(lk2), vec(sub), feat, tok, feat],
        out_specs=tok,
        out_shape=jax.ShapeDtypeStruct((b, s, D_ATTN), BF16),
        scratch_shapes=[pltpu.VMEM((SCORE_AHEAD + 1, s // K_TILE, K_TILE, 2 * Q_TILE), F32)],
        compiler_params=pltpu.CompilerParams(dimension_semantics=("arbitrary", "arbitrary"),
                                             vmem_limit_bytes=VMEM_LIMIT_BYTES),
        name="diff_attn",
    )(lq1, lk1, lq2, lk2, sub, qt, k, vt)


def _rglru_kernel(xr_ref, gate_ref, cw_ref, cb_ref, wab_ref, ba_ref, bx_ref, ap_ref, y_ref,
                  xext_ref, a_ref, b_ref, h_ref):
    t_tile = xr_ref.shape[0]
    halo = SUBLANES

    @pl.when(pl.program_id(1) == 0)
    def _():
        xext_ref[0:halo, :] = jnp.zeros((halo, D_RNN), F32)
        h_ref[...] = jnp.zeros_like(h_ref)

    xext_ref[halo:halo + t_tile, :] = xr_ref[...]
    xc = cb_ref[...]
    for j in range(CONV_WIDTH):
        off = halo - (CONV_WIDTH - 1) + j
        xc = xc + cw_ref[j:j + 1, :] * xext_ref[off:off + t_tile, :]
    xext_ref[0:halo, :] = xext_ref[t_tile:t_tile + halo, :]

    gates = jnp.dot(xc.astype(BF16), wab_ref[...], preferred_element_type=F32)
    r = jax.nn.sigmoid(gates[:, :D_RNN] + ba_ref[...])
    i = jax.nn.sigmoid(gates[:, D_RNN:] + bx_ref[...])
    neg_ap = -ap_ref[...]
    softplus = jnp.maximum(neg_ap, 0.0) + jnp.log1p(jnp.exp(-jnp.abs(neg_ap)))
    a = jnp.exp(-RG_C * r * softplus)
    a_ref[...] = a
    b_ref[...] = jnp.sqrt(1.0 - a * a) * i * xc

    row = lax.broadcasted_iota(jnp.int32, (SUBLANES, D_RNN), 0)

    def body(g, h):
        rows = pl.ds(pl.multiple_of(g * SUBLANES, SUBLANES), SUBLANES)
        a8 = a_ref[rows, :]
        b8 = b_ref[rows, :]
        for d in (1, 2, 4):
            keep = row >= d
            b8 = b8 + a8 * jnp.where(keep, pltpu.roll(b8, d, axis=0), 0.0)
            a8 = a8 * jnp.where(keep, pltpu.roll(a8, d, axis=0), 1.0)
        h8 = b8 + a8 * h
        b_ref[rows, :] = h8
        return jnp.broadcast_to(h8[SUBLANES - 1:SUBLANES, :], (SUBLANES, D_RNN))

    h_ref[...] = lax.fori_loop(0, t_tile // SUBLANES, body, h_ref[...])
    y_ref[...] = (b_ref[...] * jax.nn.gelu(gate_ref[...], approximate=True)).astype(BF16)


def _rglru(xr, gate, cw, cb, wab, ba, bx, ap):
    b, s, _ = xr.shape
    tt = TIME_TILE
    tile = pl.BlockSpec((None, tt, D_RNN), lambda bi, ti: (bi, ti, 0))
    return pl.pallas_call(
        _rglru_kernel,
        grid=(b, s // tt),
        in_specs=[tile, tile, _resident(cw.shape), _resident(cb.shape), _resident(wab.shape),
                  _resident(ba.shape), _resident(bx.shape), _resident(ap.shape)],
        out_specs=tile,
        out_shape=jax.ShapeDtypeStruct((b, s, D_RNN), BF16),
        scratch_shapes=[pltpu.VMEM((tt + SUBLANES, D_RNN), F32), pltpu.VMEM((tt, D_RNN), F32),
                        pltpu.VMEM((tt, D_RNN), F32), pltpu.VMEM((SUBLANES, D_RNN), F32)],
        compiler_params=pltpu.CompilerParams(dimension_semantics=("arbitrary", "arbitrary"),
                                             vmem_limit_bytes=VMEM_LIMIT_BYTES),
        name="rglru",
    )(xr, gate, cw, cb, wab, ba, bx, ap)


def _outproj_ffn2_kernel(x1_ref, ya_ref, yr_ref, woa_ref, wor_ref, n_ref, wg_ref, wu_ref, wo_ref,
                         out_ref, act_ref):
    x2 = (x1_ref[...]
          + jnp.dot(ya_ref[...], woa_ref[...], preferred_element_type=F32)
          + jnp.dot(yr_ref[...], wor_ref[...], preferred_element_type=F32))
    h = _rms(x2, n_ref[...]).astype(BF16)
    out_ref[...] = x2 + 0.5 * _swiglu(h, wg_ref, wu_ref, wo_ref, act_ref)


def _outproj_ffn2(x1, ya, yr, woa, wor, nrm, wg, wu, wo):
    n = x1.shape[0]
    tm = TOKEN_TILE
    row = lambda w: pl.BlockSpec((tm, w), lambda i: (i, 0))
    return pl.pallas_call(
        _outproj_ffn2_kernel,
        grid=(n // tm,),
        in_specs=[row(D_MODEL), row(D_ATTN), row(D_RNN), _resident(woa.shape), _resident(wor.shape),
                  _resident(nrm.shape), _resident(wg.shape), _resident(wu.shape), _resident(wo.shape)],
        out_specs=row(D_MODEL),
        out_shape=jax.ShapeDtypeStruct((n, D_MODEL), F32),
        scratch_shapes=[pltpu.VMEM((tm, D_FF), BF16)],
        compiler_params=pltpu.CompilerParams(dimension_semantics=("arbitrary",),
                                             vmem_limit_bytes=VMEM_LIMIT_BYTES),
        name="outproj_ffn2",
    )(x1, ya, yr, woa, wor, nrm, wg, wu, wo)


def _block_diag(w):
    nb, c, d = w.shape
    eye = jnp.eye(nb, dtype=w.dtype)
    return (eye[:, None, :, None] * w[:, :, None, :]).reshape(nb * c, nb * d)


def kernel(x, positions, ffn1_norm, ffn1_w_in, ffn1_w_out, mix_norm, w_in_proj, q_norm, k_norm, lambda_q1, lambda_k1, lambda_q2, lambda_k2, subln_norm, conv_w, conv_b, rg_w_a, rg_b_a, rg_w_x, rg_b_x, rg_a_param, w_out_proj, ffn2_norm, ffn2_w_in, ffn2_w_out):
    b, s, d = x.shape
    depth = ffn1_norm.shape[0]
    n = b * s
    tm = TOKEN_TILE
    row = lambda a: a.reshape(1, -1).astype(F32)
    col = lambda a, width: jnp.broadcast_to(a.astype(F32)[:, None], (a.shape[0], width))

    inv_freq = ROPE_THETA ** (-jnp.arange(0, ROPE_DIM, 2, dtype=F32) / ROPE_DIM)
    freq = col(inv_freq, tm)
    pos = positions.reshape(b, 1, s)

    for l in range(depth):
        lambda_init = 0.8 - 0.6 * math.exp(-0.3 * l)
        w_in1 = ffn1_w_in[l].astype(BF16)
        w_in2 = ffn2_w_in[l].astype(BF16)
        w_ip = w_in_proj[l].astype(BF16)
        w_op = w_out_proj[l].astype(BF16)
        wab = jnp.concatenate([_block_diag(rg_w_a[l]), _block_diag(rg_w_x[l])], axis=1).astype(BF16)

        x1, qt, k, vt, xr, gate = _ffn1_inproj(
            x, pos, row(ffn1_norm[l]), w_in1[:, :D_FF], w_in1[:, D_FF:], ffn1_w_out[l].astype(BF16),
            row(mix_norm[l]), w_ip[:, :D_ATTN].T, w_ip[:, D_ATTN:2 * D_ATTN].T, w_ip[:, 2 * D_ATTN:3 * D_ATTN].T,
            w_ip[:, 3 * D_ATTN:], col(q_norm[l], tm), col(k_norm[l], tm), freq)

        y_attn = _diff_attn(qt, k, vt, row(lambda_q1[l]), row(lambda_k1[l]), row(lambda_q2[l]),
                            row(lambda_k2[l]), col(subln_norm[l], Q_TILE), lambda_init)
        y_rnn = _rglru(xr, gate, conv_w[l].astype(F32), row(conv_b[l]), wab, row(rg_b_a[l]),
                       row(rg_b_x[l]), row(rg_a_param[l]))

        x = _outproj_ffn2(x1.reshape(n, d), y_attn.reshape(n, D_ATTN), y_rnn.reshape(n, D_RNN),
                          w_op[:D_ATTN], w_op[D_ATTN:], row(ffn2_norm[l]),
                          w_in2[:, :D_FF], w_in2[:, D_FF:], ffn2_w_out[l].astype(BF16)).reshape(b, s, d)
    return x
```
